```python
import jax, jax.numpy as jnp
from jax import lax
import numpy as np

D_MODEL = 1024
BATCH = 4
SEQ = 8192
DEPTH = 1
DEC_BATCH = 32
DEC_SEQ = 32
PAST_LEN = 2048


CHUNK = 64
PAST_CHUNKS = 8
ATT_WINDOW = PAST_CHUNKS * CHUNK
BAND = ATT_WINDOW + CHUNK
MIX_W = D_MODEL
ATT_W = MIX_W // 2
CONV_W = MIX_W - ATT_W
HEAD_DIM = 64
N_HEADS = ATT_W // HEAD_DIM
REL_CLIP = 128
N_REL = 2 * REL_CLIP + 1
CONV_WIDTH = 31
CONV_HIST = CONV_WIDTH - 1
D_FF = -(-8 * D_MODEL // (3 * 256)) * 256
PLE_DIM = 256
IN_COLS = 3 * ATT_W + 2 * CONV_W
EPS = 1e-6
NEG = -1e30

kernel_name = 'hybrid_chunk_attn_conformer_conv_step'


def rms_norm(x, g):
    xf = x.astype(jnp.float32)
    y = xf * lax.rsqrt(jnp.mean(xf * xf, axis=-1, keepdims=True) + EPS)
    return (y * g.astype(jnp.float32)).astype(x.dtype)


def layer_norm(x, g, b):
    xf = x.astype(jnp.float32)
    mu = jnp.mean(xf, axis=-1, keepdims=True)
    xc = xf - mu
    y = xc * lax.rsqrt(jnp.mean(xc * xc, axis=-1, keepdims=True) + EPS)
    return (y * g.astype(jnp.float32) + b.astype(jnp.float32)).astype(x.dtype)


def project_mix_inputs(h, g_mix, w_in, q_gain, k_gain):
    u = rms_norm(h, g_mix)
    z = u @ w_in
    q, k, v, a, b = jnp.split(z, [ATT_W, 2 * ATT_W, 3 * ATT_W, 3 * ATT_W + CONV_W], axis=-1)
    shp = h.shape[:-1] + (N_HEADS, HEAD_DIM)
    q = rms_norm(q.reshape(shp), q_gain)
    k = rms_norm(k.reshape(shp), k_gain)
    v = v.reshape(shp)
    glu = a * jax.nn.sigmoid(b)
    return q, k, v, glu


def band_attend(q, k, v, q_pos, k_pos, rel_bias):
    s = jnp.einsum('nqhd,nkhd->nhqk', q.astype(jnp.float32), k.astype(jnp.float32)) * (HEAD_DIM ** -0.5)
    dist = q_pos[:, None] - k_pos[None, :]
    bias = rel_bias.astype(jnp.float32)[:, jnp.clip(dist, -REL_CLIP, REL_CLIP) + REL_CLIP]
    qc = (q_pos // CHUNK)[:, None]
    kc = (k_pos // CHUNK)[None, :]
    mask = (k_pos[None, :] >= 0) & (kc <= qc) & (kc >= qc - PAST_CHUNKS)
    s = jnp.where(mask, s + bias, NEG)
    p = jax.nn.softmax(s, axis=-1).astype(v.dtype)
    return jnp.einsum('nhqk,nkhd->nqhd', p, v)


def prompt_attention(q, k, v, rel_bias):
    n, s = q.shape[0], q.shape[1]
    n_chunks = s // CHUNK
    pad = jnp.zeros((n, ATT_WINDOW, N_HEADS, HEAD_DIM), k.dtype)
    kp = jnp.concatenate([pad, k], axis=1)
    vp = jnp.concatenate([pad, v], axis=1)

    def one_chunk(c):
        start = c * CHUNK
        qc = lax.dynamic_slice_in_dim(q, start, CHUNK, axis=1)
        kc = lax.dynamic_slice_in_dim(kp, start, BAND, axis=1)
        vc = lax.dynamic_slice_in_dim(vp, start, BAND, axis=1)
        q_pos = start + jnp.arange(CHUNK)
        k_pos = start - ATT_WINDOW + jnp.arange(BAND)
        return band_attend(qc, kc, vc, q_pos, k_pos, rel_bias)

    out = lax.map(one_chunk, jnp.arange(n_chunks))
    return jnp.moveaxis(out, 0, 1).reshape(n, s, ATT_W)


def conv_module_tail(glu, hist, conv_w, conv_b, ln_g, ln_b):
    xp = jnp.concatenate([hist, glu], axis=1)
    y = lax.conv_general_dilated(xp, conv_w[:, None, :], (1,), 'VALID',
                                 dimension_numbers=('NWC', 'WIO', 'NWC'),
                                 feature_group_count=CONV_W)
    y = y + conv_b
    return jax.nn.silu(layer_norm(y, ln_g, ln_b))


def finish_layer(h, att, cv, p, w_out, g_ffn, w_gate_up, w_down, g_ple, w_ple_gate, w_ple_proj):
    h = h + jnp.concatenate([att, cv], axis=-1) @ w_out
    u = rms_norm(h, g_ffn)
    gate, up = jnp.split(u @ w_gate_up, 2, axis=-1)
    h = h + (jax.nn.silu(gate) * up) @ w_down
    g = jax.nn.sigmoid(rms_norm(h, g_ple) @ w_ple_gate)
    return h + g * (p @ w_ple_proj)


def setup_inputs(seed: int = 0) -> dict:
    key = jax.random.key(seed)
    ks = jax.random.split(key, 23)
    rows = min(ATT_WINDOW, PAST_LEN)

    def nrm(k, shape, scale):
        return jax.random.normal(k, shape, jnp.float32) * scale

    return {
        'x_prompt': nrm(ks[0], (BATCH, SEQ, D_MODEL), 1.0),
        'x_sample': nrm(ks[1], (DEC_BATCH, DEC_SEQ, D_MODEL), 1.0),
        'p_prompt': nrm(ks[2], (DEPTH, BATCH, SEQ, PLE_DIM), 1.0),
        'p_sample': nrm(ks[3], (DEPTH, DEC_BATCH, DEC_SEQ, PLE_DIM), 1.0),
        'cache_k': nrm(ks[4], (DEPTH, DEC_BATCH, rows, N_HEADS, HEAD_DIM), 1.0),
        'cache_v': nrm(ks[5], (DEPTH, DEC_BATCH, rows, N_HEADS, HEAD_DIM), 0.5),
        'state_conv': nrm(ks[6], (DEPTH, DEC_BATCH, CONV_HIST, CONV_W), 0.5),
        'g_mix': 1.0 + nrm(ks[7], (DEPTH, D_MODEL), 0.02),
        'w_in': nrm(ks[8], (DEPTH, D_MODEL, IN_COLS), D_MODEL ** -0.5),
        'q_gain': 1.0 + nrm(ks[9], (DEPTH, HEAD_DIM), 0.02),
        'k_gain': 1.0 + nrm(ks[10], (DEPTH, HEAD_DIM), 0.02),
        'rel_bias': nrm(ks[11], (DEPTH, N_HEADS, N_REL), 0.1),
        'conv_w': nrm(ks[12], (DEPTH, CONV_WIDTH, CONV_W), CONV_WIDTH ** -0.5),
        'conv_b': nrm(ks[13], (DEPTH, CONV_W), 0.01),
        'conv_ln_g': 1.0 + nrm(ks[14], (DEPTH, CONV_W), 0.02),
        'conv_ln_b': nrm(ks[15], (DEPTH, CONV_W), 0.01),
        'w_out': nrm(ks[16], (DEPTH, MIX_W, D_MODEL), MIX_W ** -0.5),
        'g_ffn': 1.0 + nrm(ks[17], (DEPTH, D_MODEL), 0.02),
        'w_gate_up': nrm(ks[18], (DEPTH, D_MODEL, 2 * D_FF), D_MODEL ** -0.5),
        'w_down': nrm(ks[19], (DEPTH, D_FF, D_MODEL), D_FF ** -0.5),
        'g_ple': 1.0 + nrm(ks[20], (DEPTH, D_MODEL), 0.02),
        'w_ple_gate': nrm(ks[21], (DEPTH, D_MODEL, D_MODEL), D_MODEL ** -0.5),
        'w_ple_proj': nrm(ks[22], (DEPTH, PLE_DIM, D_MODEL), PLE_DIM ** -0.5),
    }


def reference(x_prompt, x_sample, p_prompt, p_sample, cache_k, cache_v, state_conv,
              g_mix, w_in, q_gain, k_gain, rel_bias, conv_w, conv_b, conv_ln_g, conv_ln_b,
              w_out, g_ffn, w_gate_up, w_down, g_ple, w_ple_gate, w_ple_proj):
    hp, hs = x_prompt, x_sample
    n_p, s_p = x_prompt.shape[0], x_prompt.shape[1]
    n_s, l_s = x_sample.shape[0], x_sample.shape[1]
    keep_p = min(ATT_WINDOW, s_p)
    rows_s = cache_k.shape[2]
    kp_l, vp_l, cp_l, ks_l, vs_l, cs_l = [], [], [], [], [], []
    for i in range(DEPTH):
        q, k, v, glu = project_mix_inputs(hp, g_mix[i], w_in[i], q_gain[i], k_gain[i])
        att = prompt_attention(q, k, v, rel_bias[i])
        hist0 = jnp.zeros((n_p, CONV_HIST, CONV_W), glu.dtype)
        cv = conv_module_tail(glu, hist0, conv_w[i], conv_b[i], conv_ln_g[i], conv_ln_b[i])
        hp_next = finish_layer(hp, att, cv, p_prompt[i], w_out[i], g_ffn[i], w_gate_up[i],
                               w_down[i], g_ple[i], w_ple_gate[i], w_ple_proj[i])
        kp_l.append(k[:, s_p - keep_p:])
        vp_l.append(v[:, s_p - keep_p:])
        cp_l.append(glu[:, s_p - CONV_HIST:])
        hp = hp_next

        q, k, v, glu = project_mix_inputs(hs, g_mix[i], w_in[i], q_gain[i], k_gain[i])
        k_all = jnp.concatenate([cache_k[i], k], axis=1)
        v_all = jnp.concatenate([cache_v[i], v], axis=1)
        q_pos = PAST_LEN + jnp.arange(l_s)
        k_pos = PAST_LEN - rows_s + jnp.arange(rows_s + l_s)
        att = band_attend(q, k_all, v_all, q_pos, k_pos, rel_bias[i]).reshape(n_s, l_s, ATT_W)
        cv = conv_module_tail(glu, state_conv[i], conv_w[i], conv_b[i], conv_ln_g[i], conv_ln_b[i])
        hs_next = finish_layer(hs, att, cv, p_sample[i], w_out[i], g_ffn[i], w_gate_up[i],
                               w_down[i], g_ple[i], w_ple_gate[i], w_ple_proj[i])
        ks_l.append(k_all[:, l_s:])
        vs_l.append(v_all[:, l_s:])
        conv_all = jnp.concatenate([state_conv[i], glu], axis=1)
        cs_l.append(conv_all[:, l_s:])
        hs = hs_next

    new_k_prompt = jnp.stack(kp_l)
    new_v_prompt = jnp.stack(vp_l)
    new_conv_prompt = jnp.stack(cp_l)
    new_k_sample = jnp.stack(ks_l)
    new_v_sample = jnp.stack(vs_l)
    new_conv_sample = jnp.stack(cs_l)
    return (hp, hs, new_k_prompt, new_v_prompt, new_conv_prompt, new_k_sample, new_v_sample, new_conv_sample)
```

```python
import functools

import numpy as np
import jax
import jax.numpy as jnp
from jax import lax
from jax.experimental import pallas as pl
from jax.experimental.pallas import tpu as pltpu

F32 = jnp.float32
BF16 = jnp.bfloat16

CHUNK = 64
PAST_CHUNKS = 8
ATT_WINDOW = PAST_CHUNKS * CHUNK
HEAD_DIM = 64
REL_CLIP = 128
CONV_WIDTH = 31
CONV_HIST = CONV_WIDTH - 1
PAST_LEN = 2048
EPS = 1e-6
NEG = -1e30

LANES = 128
HIST_ROWS = 32
ROW_TILE = 512
Q_TILE = 256
VMEM_LIMIT_BYTES = 60000 * 1024


def _rms(x, g):
    return x * lax.rsqrt(jnp.mean(x * x, axis=-1, keepdims=True) + EPS) * g


def _dot(a, b):
    return jnp.dot(a, b, preferred_element_type=F32)


def _dot_nt(a, b):
    return lax.dot_general(a, b, (((1,), (1,)), ((), ())), preferred_element_type=F32)


def _resident(shape):
    nd = len(shape)
    return pl.BlockSpec(shape, lambda *_: (0,) * nd, pipeline_mode=pl.Buffered(1))


def _proj_kernel(x_ref, g_ref, w_ref, bd_ref, qg_ref, kg_ref, q_ref, k_ref, v_ref, glu_ref, *, att_w, conv_w):
    u = _rms(x_ref[...], g_ref[...]).astype(BF16)

    def head_norm(t, gain):
        sq = t * t
        hi = sq.astype(BF16)
        lo = (sq - hi.astype(F32)).astype(BF16)
        ms = _dot(hi, bd_ref[...]) + _dot(lo, bd_ref[...])
        return t * lax.rsqrt(ms + EPS) * gain

    zq = _dot(u, w_ref[:, 0:att_w])
    q_ref[...] = (head_norm(zq, qg_ref[...]) * (HEAD_DIM ** -0.5)).astype(BF16)
    zk = _dot(u, w_ref[:, att_w:2 * att_w])
    k_ref[...] = head_norm(zk, kg_ref[...])
    v_ref[...] = _dot(u, w_ref[:, 2 * att_w:3 * att_w])
    a = _dot(u, w_ref[:, 3 * att_w:3 * att_w + conv_w])
    b = _dot(u, w_ref[:, 3 * att_w + conv_w:3 * att_w + 2 * conv_w])
    glu_ref[...] = a * jax.nn.sigmoid(b)


def _proj(x2d, g_mix, w_in_bf, bd, q_gain_t, k_gain_t, att_w, conv_w):
    rows, d = x2d.shape
    tm = min(ROW_TILE, rows)
    assert rows % tm == 0
    row_spec = lambda width: pl.BlockSpec((tm, width), lambda i: (i, 0))
    return pl.pallas_call(
        functools.partial(_proj_kernel, att_w=att_w, conv_w=conv_w),
        grid=(rows // tm,),
        in_specs=[row_spec(d), _resident(g_mix.shape), _resident(w_in_bf.shape), _resident(bd.shape),
                  _resident(q_gain_t.shape), _resident(k_gain_t.shape)],
        out_specs=[row_spec(att_w), row_spec(att_w), row_spec(att_w), row_spec(conv_w)],
        out_shape=[jax.ShapeDtypeStruct((rows, att_w), BF16), jax.ShapeDtypeStruct((rows, att_w), F32),
                   jax.ShapeDtypeStruct((rows, att_w), F32), jax.ShapeDtypeStruct((rows, conv_w), F32)],
        compiler_params=pltpu.CompilerParams(dimension_semantics=("parallel",), vmem_limit_bytes=VMEM_LIMIT_BYTES),
        name="proj",
    )(x2d, g_mix, w_in_bf, bd, q_gain_t, k_gain_t)


def _softmax_pv(q2, k_blocks, v_blocks, bias_blocks, lo_half):
    outs = []
    for hh in range(2):
        sel = lo_half if hh == 0 else jnp.logical_not(lo_half)
        qm = jnp.where(sel, q2, jnp.zeros_like(q2))
        s = [_dot_nt(qm, kb) + bb for kb, bb in zip(k_blocks, bias_blocks[hh])]
        m = functools.reduce(jnp.maximum, [jnp.max(sj, axis=-1, keepdims=True) for sj in s])
        p = [jnp.exp(sj - m) for sj in s]
        l = functools.reduce(jnp.add, [jnp.sum(pj, axis=-1, keepdims=True) for pj in p])
        o = functools.reduce(jnp.add, [_dot(pj.astype(BF16), vb) for pj, vb in zip(p, v_blocks)])
        outs.append(o / l)
    return jnp.where(lo_half, outs[0], outs[1])


def _attn_prompt_kernel(q_ref, k0_ref, k1_ref, k2_ref, v0_ref, v1_ref, v2_ref, bm_ref, o_ref, *, n_heads):
    qb = pl.program_id(1)
    tq = q_ref.shape[1]
    kb = k0_ref.shape[1]
    lo_half = lax.broadcasted_iota(jnp.int32, (tq, LANES), 1) < HEAD_DIM
    k_refs = (k0_ref, k1_ref, k2_ref)
    v_refs = (v0_ref, v1_ref, v2_ref)
    nblk = len(k_refs)
    pen = [jnp.where(qb - (nblk - 1) + j >= 0, 0.0, NEG).astype(F32) for j in range(nblk - 1)] + [None]
    for pair in range(n_heads // 2):
        ls = slice(pair * LANES, (pair + 1) * LANES)
        q2 = q_ref[0, :, ls]
        k_blocks = [r[0, :, ls].astype(BF16) for r in k_refs]
        v_blocks = [r[0, :, ls].astype(BF16) for r in v_refs]
        bias_blocks = []
        for hh in range(2):
            h = 2 * pair + hh
            row = []
            for j in range(nblk):
                bb = bm_ref[h, :, j * kb:(j + 1) * kb]
                row.append(bb if pen[j] is None else bb + pen[j])
            bias_blocks.append(row)
        o_ref[0, :, ls] = _softmax_pv(q2, k_blocks, v_blocks, bias_blocks, lo_half).astype(o_ref.dtype)


def _attn_prompt(q, k, v, bm, n_heads):
    n, s, w = q.shape
    tq = Q_TILE
    nblk = (ATT_WINDOW + tq) // tq
    assert s % tq == 0 and ATT_WINDOW % tq == 0 and nblk == 3
    q_spec = pl.BlockSpec((1, tq, w), lambda b, i: (b, i, 0))
    kv_specs = [pl.BlockSpec((1, tq, w), functools.partial(lambda b, i, j: (b, jnp.maximum(i - (nblk - 1) + j, 0), 0), j=j))
                for j in range(nblk)]
    return pl.pallas_call(
        functools.partial(_attn_prompt_kernel, n_heads=n_heads),
        grid=(n, s // tq),
        in_specs=[q_spec] + kv_specs + kv_specs + [_resident(bm.shape)],
        out_specs=q_spec,
        out_shape=jax.ShapeDtypeStruct((n, s, w), BF16),
        compiler_params=pltpu.CompilerParams(dimension_semantics=("parallel", "parallel"),
                                             vmem_limit_bytes=VMEM_LIMIT_BYTES),
        name="attn_prompt",
    )(q, k, k, k, v, v, v, bm)


def _attn_sample_kernel(q_ref, ck_ref, kn_ref, cv_ref, vn_ref, bc_ref, bn_ref, o_ref, nk_ref, nv_ref, *, n_heads):
    lq = q_ref.shape[1]
    rows = ck_ref.shape[1]
    lo_half = lax.broadcasted_iota(jnp.int32, (lq, LANES), 1) < HEAD_DIM
    for pair in range(n_heads // 2):
        ls = slice(pair * LANES, (pair + 1) * LANES)
        k_blocks = [ck_ref[0, :, ls].astype(BF16), kn_ref[0, :, ls].astype(BF16)]
        v_blocks = [cv_ref[0, :, ls].astype(BF16), vn_ref[0, :, ls].astype(BF16)]
        bias_blocks = [[bc_ref[2 * pair + hh], bn_ref[2 * pair + hh]] for hh in range(2)]
        o_ref[0, :, ls] = _softmax_pv(q_ref[0, :, ls], k_blocks, v_blocks, bias_blocks, lo_half).astype(o_ref.dtype)
    nk_ref[0, 0:rows - lq, :] = ck_ref[0, lq:rows, :]
    nk_ref[0, rows - lq:rows, :] = kn_ref[0]
    nv_ref[0, 0:rows - lq, :] = cv_ref[0, lq:rows, :]
    nv_ref[0, rows - lq:rows, :] = vn_ref[0]


def _attn_sample(q, cache_k, k_new, cache_v, v_new, bias_c, bias_n, n_heads):
    n, lq, w = q.shape
    rows = cache_k.shape[1]
    assert lq <= rows and lq % 8 == 0
    new_spec = pl.BlockSpec((1, lq, w), lambda b: (b, 0, 0))
    cache_spec = pl.BlockSpec((1, rows, w), lambda b: (b, 0, 0))
    return pl.pallas_call(
        functools.partial(_attn_sample_kernel, n_heads=n_heads),
        grid=(n,),
        in_specs=[new_spec, cache_spec, new_spec, cache_spec, new_spec, _resident(bias_c.shape), _resident(bias_n.shape)],
        out_specs=[new_spec, cache_spec, cache_spec],
        out_shape=[jax.ShapeDtypeStruct((n, lq, w), BF16), jax.ShapeDtypeStruct((n, rows, w), F32),
                   jax.ShapeDtypeStruct((n, rows, w), F32)],
        compiler_params=pltpu.CompilerParams(dimension_semantics=("parallel",), vmem_limit_bytes=VMEM_LIMIT_BYTES),
        name="attn_sample",
    )(q, cache_k, k_new, cache_v, v_new, bias_c, bias_n)


def _ffn_chunks(d_ff):
    step = 1024
    return [(c, min(c + step, d_ff)) for c in range(0, d_ff, step)]


def _tail_kernel(x_ref, att_ref, glu_ref, hist_ref, p_ref, cw_ref, cb_ref, lng_ref, lnb_ref, wout_ref, gffn_ref,
                 wgu_ref, wdn_ref, gple_ref, wpg_ref, wpp_ref, y_ref, gbuf, *, nseg, seg_len, tiles_per_seq):
    i = pl.program_id(0)
    att_w = att_ref.shape[1]
    d_ff = wdn_ref.shape[0]
    off = HIST_ROWS - CONV_HIST

    conv_parts = []
    for s in range(nseg):
        hist = hist_ref[s]
        if tiles_per_seq is not None:
            hist = jnp.where(i % tiles_per_seq == 0, jnp.zeros_like(hist), hist)
        gbuf[s, 0:HIST_ROWS, :] = hist
        gbuf[s, HIST_ROWS:HIST_ROWS + seg_len, :] = glu_ref[s * seg_len:(s + 1) * seg_len, :]
        acc = gbuf[s, off:off + seg_len, :] * cw_ref[0:1, :]
        for w in range(1, CONV_WIDTH):
            acc = acc + gbuf[s, off + w:off + w + seg_len, :] * cw_ref[w:w + 1, :]
        conv_parts.append(acc)
    y = (conv_parts[0] if nseg == 1 else jnp.concatenate(conv_parts, axis=0)) + cb_ref[...]
    mu = jnp.mean(y, axis=-1, keepdims=True)
    yc = y - mu
    ln = yc * lax.rsqrt(jnp.mean(yc * yc, axis=-1, keepdims=True) + EPS) * lng_ref[...] + lnb_ref[...]
    cv = jax.nn.silu(ln).astype(BF16)

    h = x_ref[...] + _dot(att_ref[...], wout_ref[0:att_w, :]) + _dot(cv, wout_ref[att_w:, :])

    u = _rms(h, gffn_ref[...]).astype(BF16)
    ffn = None
    for c0, c1 in _ffn_chunks(d_ff):
        gate = _dot(u, wgu_ref[:, c0:c1])
        up = _dot(u, wgu_ref[:, d_ff + c0:d_ff + c1])
        part = _dot((jax.nn.silu(gate) * up).astype(BF16), wdn_ref[c0:c1, :])
        ffn = part if ffn is None else ffn + part
    h = h + ffn

    g = jax.nn.sigmoid(_dot(_rms(h, gple_ref[...]).astype(BF16), wpg_ref[...]))
    y_ref[...] = h + g * _dot(p_ref[...].astype(BF16), wpp_ref[...])


def _tail(x2d, att2d, glu2d, hist, p2d, params, *, seg_len, tiles_per_seq):
    rows, d = x2d.shape
    tm = min(ROW_TILE, rows)
    assert rows % tm == 0 and tm % seg_len == 0
    nseg = tm // seg_len
    row_spec = lambda width: pl.BlockSpec((tm, width), lambda i: (i, 0))
    if tiles_per_seq is None:
        hist_spec = pl.BlockSpec((nseg, HIST_ROWS, hist.shape[2]), lambda i: (i, 0, 0))
    else:
        assert nseg == 1
        per_tile = tm // HIST_ROWS
        hist_spec = pl.BlockSpec((1, HIST_ROWS, hist.shape[2]), lambda i: (jnp.maximum(i * per_tile - 1, 0), 0, 0))
    return pl.pallas_call(
        functools.partial(_tail_kernel, nseg=nseg, seg_len=seg_len, tiles_per_seq=tiles_per_seq),
        grid=(rows // tm,),
        in_specs=[row_spec(d), row_spec(att2d.shape[1]), row_spec(glu2d.shape[1]), hist_spec, row_spec(p2d.shape[1])]
                 + [_resident(w.shape) for w in params],
        out_specs=row_spec(d),
        out_shape=jax.ShapeDtypeStruct((rows, d), F32),
        scratch_shapes=[pltpu.VMEM((nseg, HIST_ROWS + seg_len, glu2d.shape[1]), F32)],
        compiler_params=pltpu.CompilerParams(dimension_semantics=("parallel",), vmem_limit_bytes=VMEM_LIMIT_BYTES),
        name="tail",
    )(x2d, att2d, glu2d, hist, p2d, *params)


def _rel_bias_table(rel_bias, nq, nk):
    dist = np.arange(nq)[:, None] + ATT_WINDOW - np.arange(nk)[None, :]
    idx = np.clip(dist, -REL_CLIP, REL_CLIP) + REL_CLIP
    return rel_bias.astype(F32)[:, idx]


def _band_penalty(nq, nk):
    qc = (np.arange(nq)[:, None] + ATT_WINDOW) // CHUNK
    kc = np.arange(nk)[None, :] // CHUNK
    return np.where((kc <= qc) & (kc >= qc - PAST_CHUNKS), 0.0, NEG).astype(np.float32)


def kernel(x_prompt, x_sample, p_prompt, p_sample, cache_k, cache_v, state_conv, g_mix, w_in, q_gain, k_gain, rel_bias,
           conv_w, conv_b, conv_ln_g, conv_ln_b, w_out, g_ffn, w_gate_up, w_down, g_ple, w_ple_gate, w_ple_proj):
    depth = g_mix.shape[0]
    n_p, s_p, d = x_prompt.shape
    n_s, l_s, _ = x_sample.shape
    n_heads, head_dim = cache_k.shape[3], cache_k.shape[4]
    att_w = n_heads * head_dim
    conv_wd = state_conv.shape[3]
    rows_s = cache_k.shape[2]
    keep_p = min(ATT_WINDOW, s_p)
    assert head_dim == HEAD_DIM and att_w % LANES == 0 and s_p % ROW_TILE == 0
    assert PAST_LEN % CHUNK == 0 and l_s <= CHUNK and rows_s == ATT_WINDOW and PAST_LEN >= rows_s

    bd = jnp.asarray(np.kron(np.eye(n_heads), np.full((head_dim, head_dim), 1.0 / head_dim)), BF16)
    band = _band_penalty(Q_TILE, ATT_WINDOW + Q_TILE)

    hp = x_prompt.reshape(n_p * s_p, d)
    hs = x_sample.reshape(n_s * l_s, d)
    outs = [[] for _ in range(6)]
    for i in range(depth):
        row = lambda a: a[i].reshape(1, -1).astype(F32)
        w_in_bf = w_in[i].astype(BF16)
        qg = jnp.tile(q_gain[i].astype(F32), n_heads).reshape(1, att_w)
        kg = jnp.tile(k_gain[i].astype(F32), n_heads).reshape(1, att_w)
        tail_params = (conv_w[i].astype(F32), row(conv_b), row(conv_ln_g), row(conv_ln_b), w_out[i].astype(BF16),
                       row(g_ffn), w_gate_up[i].astype(BF16), w_down[i].astype(BF16), row(g_ple),
                       w_ple_gate[i].astype(BF16), w_ple_proj[i].astype(BF16))
        bias = _rel_bias_table(rel_bias[i], Q_TILE, ATT_WINDOW + Q_TILE)

        q, k, v, glu = _proj(hp, row(g_mix), w_in_bf, bd, qg, kg, att_w, conv_wd)
        att = _attn_prompt(q.reshape(n_p, s_p, att_w), k.reshape(n_p, s_p, att_w), v.reshape(n_p, s_p, att_w),
                           bias + band, n_heads)
        hp = _tail(hp, att.reshape(n_p * s_p, att_w), glu, glu.reshape(-1, HIST_ROWS, conv_wd),
                   p_prompt[i].reshape(n_p * s_p, -1), tail_params, seg_len=ROW_TILE, tiles_per_seq=s_p // ROW_TILE)
        outs[0].append(k.reshape(n_p, s_p, n_heads, head_dim)[:, s_p - keep_p:])
        outs[1].append(v.reshape(n_p, s_p, n_heads, head_dim)[:, s_p - keep_p:])
        outs[2].append(glu.reshape(n_p, s_p, conv_wd)[:, s_p - CONV_HIST:])

        q, k, v, glu = _proj(hs, row(g_mix), w_in_bf, bd, qg, kg, att_w, conv_wd)
        att, new_k, new_v = _attn_sample(
            q.reshape(n_s, l_s, att_w), cache_k[i].reshape(n_s, rows_s, att_w), k.reshape(n_s, l_s, att_w),
            cache_v[i].reshape(n_s, rows_s, att_w), v.reshape(n_s, l_s, att_w),
            bias[:, :l_s, :rows_s], bias[:, :l_s, rows_s:rows_s + l_s], n_heads)
        hist = jnp.pad(state_conv[i], ((0, 0), (HIST_ROWS - CONV_HIST, 0), (0, 0)))
        hs = _tail(hs, att.reshape(n_s * l_s, att_w), glu, hist, p_sample[i].reshape(n_s * l_s, -1), tail_params,
                   seg_len=l_s, tiles_per_seq=None)
        outs[3].append(new_k.reshape(n_s, rows_s, n_heads, head_dim))
        outs[4].append(new_v.reshape(n_s, rows_s, n_heads, head_dim))
        outs[5].append(jnp.concatenate([state_conv[i], glu.reshape(n_s, l_s, conv_wd)], axis=1)[:, l_s:])

    return (hp.reshape(n_p, s_p, d), hs.reshape(n_s, l_s, d)) + tuple(jnp.stack(o) for o in outs)
```

```python
import functools
import math

import numpy as np
import jax
import jax.numpy as jnp
from jax import lax
from jax.experimental import pallas as pl
from jax.experimental.pallas import tpu as pltpu

F32 = jnp.float32
BF16 = jnp.bfloat16

CHUNK = 64
PAST_CHUNKS = 8
ATT_WINDOW = PAST_CHUNKS * CHUNK
HEAD_DIM = 64
REL_CLIP = 128
CONV_WIDTH = 31
CONV_HIST = CONV_WIDTH - 1
PAST_LEN = 2048
EPS = 1e-6
NEG = -1e30
LOG2E = math.log2(math.e)

LANES = 128
SUBLANES = 8
MXU_DIM = 256
HIST_ROWS = 32
ROW_TILE = 512
ROW_SPLIT = 2
Q_TILE = 256
VMEM_LIMIT_BYTES = 60000 * 1024


def _rms(x, g):
    return x * lax.rsqrt(jnp.mean(x * x, axis=-1, keepdims=True) + EPS) * g


def _dot(a, b):
    return jnp.dot(a, b, preferred_element_type=F32)


def _dot_nt(a, b):
    return lax.dot_general(a, b, (((1,), (1,)), ((), ())), preferred_element_type=F32)


def _resident(shape):
    nd = len(shape)
    return pl.BlockSpec(shape, lambda *_: (0,) * nd, pipeline_mode=pl.Buffered(1))


def _proj_kernel(x_ref, g_ref, w_ref, bd_ref, qg_ref, kg_ref, q_ref, k_ref, v_ref, glu_ref, kt_ref, vt_ref,
                 *, att_w, conv_w, tail_every):
    tm = x_ref.shape[0]
    sub = tm // ROW_SPLIT

    def head_norm(t, gain):
        sq = t * t
        hi = sq.astype(BF16)
        lo = (sq - hi.astype(F32)).astype(BF16)
        ms = jnp.concatenate(
            [_dot(hi[:, c:c + MXU_DIM], bd_ref[...]) + _dot(lo[:, c:c + MXU_DIM], bd_ref[...])
             for c in range(0, att_w, MXU_DIM)], axis=1)
        return t * lax.rsqrt(ms + EPS) * gain

    is_tail = (pl.program_id(0) % tail_every) == tail_every - 1
    for h in range(ROW_SPLIT):
        rs = slice(h * sub, (h + 1) * sub)
        u = _rms(x_ref[rs, :], g_ref[...]).astype(BF16)
        zq = _dot(u, w_ref[:, 0:att_w])
        q_ref[rs, :] = (head_norm(zq, qg_ref[...]) * (HEAD_DIM ** -0.5 * LOG2E)).astype(BF16)
        k = head_norm(_dot(u, w_ref[:, att_w:2 * att_w]), kg_ref[...])
        v = _dot(u, w_ref[:, 2 * att_w:3 * att_w])
        k_ref[rs, :] = k.astype(BF16)
        v_ref[rs, :] = v.astype(BF16)

        @pl.when(is_tail)
        def _():
            kt_ref[rs, :] = k
            vt_ref[rs, :] = v

        a = _dot(u, w_ref[:, 3 * att_w:3 * att_w + conv_w])
        b = _dot(u, w_ref[:, 3 * att_w + conv_w:3 * att_w + 2 * conv_w])
        glu_ref[rs, :] = a * jax.nn.sigmoid(b)


def _proj(x2d, g_mix, w_in_bf, bd, q_gain_t, k_gain_t, att_w, conv_w, tail_every):
    rows, d = x2d.shape
    tm = min(ROW_TILE, rows)
    assert rows % tm == 0 and (rows // tm) % tail_every == 0 and tm % (ROW_SPLIT * SUBLANES) == 0
    n_tail = rows // tm // tail_every
    row_spec = lambda width: pl.BlockSpec((tm, width), lambda i: (i, 0))
    tail_spec = pl.BlockSpec((tm, att_w), lambda i: (i // tail_every, 0))
    return pl.pallas_call(
        functools.partial(_proj_kernel, att_w=att_w, conv_w=conv_w, tail_every=tail_every),
        grid=(rows // tm,),
        in_specs=[row_spec(d), _resident(g_mix.shape), _resident(w_in_bf.shape), _resident(bd.shape),
                  _resident(q_gain_t.shape), _resident(k_gain_t.shape)],
        out_specs=[row_spec(att_w), row_spec(att_w), row_spec(att_w), row_spec(conv_w), tail_spec, tail_spec],
        out_shape=[jax.ShapeDtypeStruct((rows, att_w), BF16), jax.ShapeDtypeStruct((rows, att_w), BF16),
                   jax.ShapeDtypeStruct((rows, att_w), BF16), jax.ShapeDtypeStruct((rows, conv_w), F32),
                   jax.ShapeDtypeStruct((n_tail * tm, att_w), F32), jax.ShapeDtypeStruct((n_tail * tm, att_w), F32)],
        compiler_params=pltpu.CompilerParams(dimension_semantics=("arbitrary",), vmem_limit_bytes=VMEM_LIMIT_BYTES),
        name="proj",
    )(x2d, g_mix, w_in_bf, bd, q_gain_t, k_gain_t)


def _softmax_pv(q2, k_blocks, v_blocks, bias_blocks, lo_half):
    outs = []
    for hh in range(2):
        sel = lo_half if hh == 0 else jnp.logical_not(lo_half)
        qm = jnp.where(sel, q2, jnp.zeros_like(q2))
        s = [_dot_nt(qm, kb) + bb for kb, bb in zip(k_blocks, bias_blocks[hh])]
        groups = [sj[:, c:c + LANES] for sj in s for c in range(0, sj.shape[1], LANES) if c + LANES <= sj.shape[1]]
        rest = [sj[:, sj.shape[1] - sj.shape[1] % LANES:] for sj in s if sj.shape[1] % LANES]
        m = jnp.max(functools.reduce(jnp.maximum, groups), axis=-1, keepdims=True)
        for r in rest:
            m = jnp.maximum(m, jnp.max(r, axis=-1, keepdims=True))
        def with_ones(vb):
            lane = lax.broadcasted_iota(jnp.int32, vb.shape, 1)
            return jnp.where(lane < HEAD_DIM if hh == 0 else lane >= HEAD_DIM, vb, jnp.ones_like(vb))

        o = functools.reduce(jnp.add, [
            _dot(jnp.exp2(sj - m).astype(BF16), with_ones(vb)) for sj, vb in zip(s, v_blocks)])
        outs.append(o / pltpu.roll(o, HEAD_DIM, 1))
    return jnp.where(lo_half, outs[0], outs[1])


def _attn_prompt_kernel(q_ref, k0_ref, k1_ref, k2_ref, v0_ref, v1_ref, v2_ref, bm_ref, o_ref, *, n_heads):
    qb = pl.program_id(1)
    tq = q_ref.shape[1]
    kb = k0_ref.shape[1]
    lo_half = lax.broadcasted_iota(jnp.int32, (tq, LANES), 1) < HEAD_DIM
    k_refs = (k0_ref, k1_ref, k2_ref)
    v_refs = (v0_ref, v1_ref, v2_ref)
    nblk = len(k_refs)
    pen = [jnp.where(qb - (nblk - 1) + j >= 0, 0.0, NEG).astype(F32) for j in range(nblk - 1)] + [None]
    for pair in range(n_heads // 2):
        ls = slice(pair * LANES, (pair + 1) * LANES)
        bias_blocks = []
        for hh in range(2):
            h = 2 * pair + hh
            row = []
            for j in range(nblk):
                bb = bm_ref[h, :, j * kb:(j + 1) * kb]
                row.append(bb if pen[j] is None else bb + pen[j])
            bias_blocks.append(row)
        o_ref[0, :, ls] = _softmax_pv(q_ref[0, :, ls], [r[0, :, ls] for r in k_refs], [r[0, :, ls] for r in v_refs],
                                      bias_blocks, lo_half).astype(o_ref.dtype)


def _attn_prompt(q, k, v, bm, n_heads):
    n, s, w = q.shape
    tq = Q_TILE
    nblk = (ATT_WINDOW + tq) // tq
    assert s % tq == 0 and ATT_WINDOW % tq == 0 and nblk == 3
    q_spec = pl.BlockSpec((1, tq, w), lambda b, i: (b, i, 0))
    kv_specs = [pl.BlockSpec((1, tq, w), functools.partial(lambda b, i, j: (b, jnp.maximum(i - (nblk - 1) + j, 0), 0), j=j))
                for j in range(nblk)]
    return pl.pallas_call(
        functools.partial(_attn_prompt_kernel, n_heads=n_heads),
        grid=(n, s // tq),
        in_specs=[q_spec] + kv_specs + kv_specs + [_resident(bm.shape)],
        out_specs=q_spec,
        out_shape=jax.ShapeDtypeStruct((n, s, w), BF16),
        compiler_params=pltpu.CompilerParams(dimension_semantics=("parallel", "parallel"),
                                             vmem_limit_bytes=VMEM_LIMIT_BYTES),
        name="attn_prompt",
    )(q, k, k, k, v, v, v, bm)


def _attn_sample_kernel(q_ref, ck_ref, kn_ref, cv_ref, vn_ref, bc_ref, bn_ref, o_ref, nk_ref, nv_ref, *, n_heads):
    lq = q_ref.shape[1]
    rows = ck_ref.shape[1]
    lo_half = lax.broadcasted_iota(jnp.int32, (lq, LANES), 1) < HEAD_DIM
    for pair in range(n_heads // 2):
        ls = slice(pair * LANES, (pair + 1) * LANES)
        k_blocks = [ck_ref[0, :, ls].astype(BF16), kn_ref[0, :, ls].astype(BF16)]
        v_blocks = [cv_ref[0, :, ls].astype(BF16), vn_ref[0, :, ls].astype(BF16)]
        bias_blocks = [[bc_ref[2 * pair + hh], bn_ref[2 * pair + hh]] for hh in range(2)]
        o_ref[0, :, ls] = _softmax_pv(q_ref[0, :, ls], k_blocks, v_blocks, bias_blocks, lo_half).astype(o_ref.dtype)
    nk_ref[0, 0:rows - lq, :] = ck_ref[0, lq:rows, :]
    nk_ref[0, rows - lq:rows, :] = kn_ref[0]
    nv_ref[0, 0:rows - lq, :] = cv_ref[0, lq:rows, :]
    nv_ref[0, rows - lq:rows, :] = vn_ref[0]


def _attn_sample(q, cache_k, k_new, cache_v, v_new, bias_c, bias_n, n_heads):
    n, lq, w = q.shape
    rows = cache_k.shape[1]
    assert lq <= rows and lq % SUBLANES == 0
    new_spec = pl.BlockSpec((1, lq, w), lambda b: (b, 0, 0))
    cache_spec = pl.BlockSpec((1, rows, w), lambda b: (b, 0, 0))
    return pl.pallas_call(
        functools.partial(_attn_sample_kernel, n_heads=n_heads),
        grid=(n,),
        in_specs=[new_spec, cache_spec, new_spec, cache_spec, new_spec, _resident(bias_c.shape), _resident(bias_n.shape)],
        out_specs=[new_spec, cache_spec, cache_spec],
        out_shape=[jax.ShapeDtypeStruct((n, lq, w), BF16), jax.ShapeDtypeStruct((n, rows, w), F32),
                   jax.ShapeDtypeStruct((n, rows, w), F32)],
        compiler_params=pltpu.CompilerParams(dimension_semantics=("parallel",), vmem_limit_bytes=VMEM_LIMIT_BYTES),
        name="attn_sample",
    )(q, cache_k, k_new, cache_v, v_new, bias_c, bias_n)


def _ffn_chunks(d_ff):
    step = 4 * MXU_DIM
    return [(c, min(c + step, d_ff)) for c in range(0, d_ff, step)]


def _conv_segment(seg_buf, cw8_ref, seg_len):
    off = HIST_ROWS - CONV_HIST
    n_rows = HIST_ROWS + seg_len
    phases = [seg_buf[pl.ds(ph, n_rows - (SUBLANES if ph else 0)), :] for ph in range(SUBLANES)]
    groups = []
    for g in range(seg_len // SUBLANES):
        acc = None
        for w in range(CONV_WIDTH):
            blk, ph = divmod(w + off, SUBLANES)
            r0 = (blk + g) * SUBLANES
            term = phases[ph][r0:r0 + SUBLANES, :] * cw8_ref[w * SUBLANES:(w + 1) * SUBLANES, :]
            acc = term if acc is None else acc + term
        groups.append(acc)
    return groups


def _tail_kernel(x_ref, att_ref, glu_ref, hist_ref, p_ref, cw8_ref, cb_ref, lng_ref, lnb_ref, wout_ref, gffn_ref,
                 wgu_ref, wdn_ref, gple_ref, wpg_ref, wpp_ref, y_ref, gbuf, *, nseg, seg_len, tiles_per_seq):
    i = pl.program_id(0)
    att_w = att_ref.shape[1]
    d_ff = wdn_ref.shape[0]
    tm = nseg * seg_len
    sub = tm // ROW_SPLIT

    groups = []
    for s in range(nseg):
        hist = hist_ref[s]
        if tiles_per_seq is not None:
            hist = jnp.where(i % tiles_per_seq == 0, jnp.zeros_like(hist), hist)
        gbuf[s, 0:HIST_ROWS, :] = hist
        gbuf[s, HIST_ROWS:HIST_ROWS + seg_len, :] = glu_ref[s * seg_len:(s + 1) * seg_len, :]
        groups += _conv_segment(gbuf.at[s], cw8_ref, seg_len)

    for hf in range(ROW_SPLIT):
        rs = slice(hf * sub, (hf + 1) * sub)
        y = jnp.concatenate(groups[hf * sub // SUBLANES:(hf + 1) * sub // SUBLANES], axis=0) + cb_ref[...]
        mu = jnp.mean(y, axis=-1, keepdims=True)
        yc = y - mu
        ln = yc * lax.rsqrt(jnp.mean(yc * yc, axis=-1, keepdims=True) + EPS) * lng_ref[...] + lnb_ref[...]
        cv = jax.nn.silu(ln).astype(BF16)

        h = x_ref[rs, :] + _dot(att_ref[rs, :], wout_ref[0:att_w, :]) + _dot(cv, wout_ref[att_w:, :])

        u = _rms(h, gffn_ref[...]).astype(BF16)
        ffn = None
        for c0, c1 in _ffn_chunks(d_ff):
            gate = _dot(u, wgu_ref[:, c0:c1])
            up = _dot(u, wgu_ref[:, d_ff + c0:d_ff + c1])
            part = _dot((jax.nn.silu(gate) * up).astype(BF16), wdn_ref[c0:c1, :])
            ffn = part if ffn is None else ffn + part
        h = h + ffn

        g = jax.nn.sigmoid(_dot(_rms(h, gple_ref[...]).astype(BF16), wpg_ref[...]))
        y_ref[rs, :] = h + g * _dot(p_ref[rs, :].astype(BF16), wpp_ref[...])


def _tail(x2d, att2d, glu2d, hist, p2d, params, *, seg_len, tiles_per_seq):
    rows, d = x2d.shape
    tm = min(ROW_TILE, rows)
    assert rows % tm == 0 and tm % seg_len == 0 and (tm // ROW_SPLIT) % seg_len == 0 or seg_len == tm
    nseg = tm // seg_len
    row_spec = lambda width: pl.BlockSpec((tm, width), lambda i: (i, 0))
    if tiles_per_seq is None:
        hist_spec = pl.BlockSpec((nseg, HIST_ROWS, hist.shape[2]), lambda i: (i, 0, 0))
    else:
        assert nseg == 1
        per_tile = tm // HIST_ROWS
        hist_spec = pl.BlockSpec((1, HIST_ROWS, hist.shape[2]), lambda i: (jnp.maximum(i * per_tile - 1, 0), 0, 0))
    return pl.pallas_call(
        functools.partial(_tail_kernel, nseg=nseg, seg_len=seg_len, tiles_per_seq=tiles_per_seq),
        grid=(rows // tm,),
        in_specs=[row_spec(d), row_spec(att2d.shape[1]), row_spec(glu2d.shape[1]), hist_spec, row_spec(p2d.shape[1])]
                 + [_resident(w.shape) for w in params],
        out_specs=row_spec(d),
        out_shape=jax.ShapeDtypeStruct((rows, d), F32),
        scratch_shapes=[pltpu.VMEM((nseg, HIST_ROWS + seg_len, glu2d.shape[1]), F32)],
        compiler_params=pltpu.CompilerParams(dimension_semantics=("parallel",), vmem_limit_bytes=VMEM_LIMIT_BYTES),
        name="tail",
    )(x2d, att2d, glu2d, hist, p2d, *params)


def _rel_bias_table(rel_bias, nq, nk):
    n_rel = 2 * REL_CLIP + 1
    c = nq - 1
    lg = nk + c + 1
    n_left = ATT_WINDOW + c - REL_CLIP
    n_right = max(lg - n_left - n_rel, 0)
    rb = rel_bias.astype(F32)
    g = jnp.concatenate([jnp.repeat(rb[:, -1:], n_left, axis=1), rb[:, ::-1], jnp.repeat(rb[:, :1], n_right, axis=1)],
                        axis=1)[:, :lg]
    shifted = jnp.tile(g, (1, nq))[:, :nq * (lg - 1)].reshape(rb.shape[0], nq, lg - 1)
    return shifted[:, :, c:c + nk]


def _band_penalty(nq, nk):
    qc = (np.arange(nq)[:, None] + ATT_WINDOW) // CHUNK
    kc = np.arange(nk)[None, :] // CHUNK
    return np.where((kc <= qc) & (kc >= qc - PAST_CHUNKS), 0.0, NEG).astype(np.float32)


def kernel(x_prompt, x_sample, p_prompt, p_sample, cache_k, cache_v, state_conv, g_mix, w_in, q_gain, k_gain, rel_bias,
           conv_w, conv_b, conv_ln_g, conv_ln_b, w_out, g_ffn, w_gate_up, w_down, g_ple, w_ple_gate, w_ple_proj):
    depth = g_mix.shape[0]
    n_p, s_p, d = x_prompt.shape
    n_s, l_s, _ = x_sample.shape
    n_heads, head_dim = cache_k.shape[3], cache_k.shape[4]
    att_w = n_heads * head_dim
    conv_wd = state_conv.shape[3]
    rows_s = cache_k.shape[2]
    keep_p = min(ATT_WINDOW, s_p)
    assert head_dim == HEAD_DIM and att_w % MXU_DIM == 0 and s_p % ROW_TILE == 0 and keep_p == ROW_TILE
    assert PAST_LEN % CHUNK == 0 and l_s <= CHUNK and rows_s == ATT_WINDOW and PAST_LEN >= rows_s
    assert ROW_TILE % l_s == 0 and (n_s * l_s) % ROW_TILE == 0

    bd = jnp.asarray(np.kron(np.eye(MXU_DIM // head_dim), np.full((head_dim, head_dim), 1.0 / head_dim)), BF16)
    band = _band_penalty(Q_TILE, ATT_WINDOW + Q_TILE)

    hp = x_prompt.reshape(n_p * s_p, d)
    hs = x_sample.reshape(n_s * l_s, d)
    outs = [[] for _ in range(6)]
    for i in range(depth):
        row = lambda a: a[i].reshape(1, -1).astype(F32)
        w_in_bf = w_in[i].astype(BF16)
        qg = jnp.tile(q_gain[i].astype(F32), n_heads).reshape(1, att_w)
        kg = jnp.tile(k_gain[i].astype(F32), n_heads).reshape(1, att_w)
        cw8 = jnp.repeat(conv_w[i].astype(F32), SUBLANES, axis=0)
        tail_params = (cw8, row(conv_b), row(conv_ln_g), row(conv_ln_b), w_out[i].astype(BF16),
                       row(g_ffn), w_gate_up[i].astype(BF16), w_down[i].astype(BF16), row(g_ple),
                       w_ple_gate[i].astype(BF16), w_ple_proj[i].astype(BF16))
        bias = _rel_bias_table(rel_bias[i], Q_TILE, ATT_WINDOW + Q_TILE) * LOG2E

        q, k, v, glu, k_last, v_last = _proj(hp, row(g_mix), w_in_bf, bd, qg, kg, att_w, conv_wd, s_p // ROW_TILE)
        att = _attn_prompt(q.reshape(n_p, s_p, att_w), k.reshape(n_p, s_p, att_w), v.reshape(n_p, s_p, att_w),
                           bias + band, n_heads)
        hp = _tail(hp, att.reshape(n_p * s_p, att_w), glu, glu.reshape(-1, HIST_ROWS, conv_wd),
                   p_prompt[i].reshape(n_p * s_p, -1), tail_params, seg_len=ROW_TILE, tiles_per_seq=s_p // ROW_TILE)
        outs[0].append(k_last.reshape(n_p, keep_p, n_heads, head_dim))
        outs[1].append(v_last.reshape(n_p, keep_p, n_heads, head_dim))
        outs[2].append(glu.reshape(n_p, s_p, conv_wd)[:, s_p - CONV_HIST:])

        q, _, _, glu, k, v = _proj(hs, row(g_mix), w_in_bf, bd, qg, kg, att_w, conv_wd, 1)
        att, new_k, new_v = _attn_sample(
            q.reshape(n_s, l_s, att_w), cache_k[i].reshape(n_s, rows_s, att_w), k.reshape(n_s, l_s, att_w),
            cache_v[i].reshape(n_s, rows_s, att_w), v.reshape(n_s, l_s, att_w),
            bias[:, :l_s, :rows_s], bias[:, :l_s, rows_s:rows_s + l_s], n_heads)
        hist = jnp.pad(state_conv[i], ((0, 0), (HIST_ROWS - CONV_HIST, 0), (0, 0)))
        hs = _tail(hs, att.reshape(n_s * l_s, att_w), glu, hist, p_sample[i].reshape(n_s * l_s, -1), tail_params,
                   seg_len=l_s, tiles_per_seq=None)
        outs[3].append(new_k.reshape(n_s, rows_s, n_heads, head_dim))
        outs[4].append(new_v.reshape(n_s, rows_s, n_heads, head_dim))
        outs[5].append(jnp.concatenate([state_conv[i], glu.reshape(n_s, l_s, conv_wd)], axis=1)[:, l_s:])

    return (hp.reshape(n_p, s_p, d), hs.reshape(n_s, l_s, d)) + tuple(jnp.stack(o) for o in outs)
```
